```python
import math
import jax, jax.numpy as jnp
from jax import lax
import numpy as np

D_MODEL = 1024
BATCH = 4
SEQ = 4096
DEPTH = 4
DEC_BATCH = 128
DEC_SEQ = 1
PAST_LEN = 2048
PAGE_SIZE = 128

N_MIXERS = 3
POOL_WINDOWS = (2, 4, 8, 16)
N_POOL_GROUPS = len(POOL_WINDOWS)
POOL_GROUP = D_MODEL // N_POOL_GROUPS
POOL_HIST = max(POOL_WINDOWS) - 1
CONV_WIDTH = 31
CONV_HIST = CONV_WIDTH - 1
N_HEADS = 8
HEAD_DIM = D_MODEL // N_HEADS
QK_DIM = HEAD_DIM // 2
D_FF = -(-8 * D_MODEL // (3 * 256)) * 256
QBLOCK = 128
NORM_EPS = 1e-6
MASK_VALUE = -1e30
N_POOL_LAYERS = len(range(0, DEPTH, N_MIXERS))
N_CONV_LAYERS = len(range(1, DEPTH, N_MIXERS))
N_ATTN_LAYERS = len(range(2, DEPTH, N_MIXERS))

kernel_name = "pool_conv_diffattn_hybrid_step"


def rmsnorm(x, g):
    xf = x.astype(jnp.float32)
    y = xf * lax.rsqrt(jnp.mean(xf * xf, axis=-1, keepdims=True) + NORM_EPS)
    return (y * g.astype(jnp.float32)).astype(x.dtype)


def layernorm(x, g, b):
    xf = x.astype(jnp.float32)
    mu = jnp.mean(xf, axis=-1, keepdims=True)
    xc = xf - mu
    y = xc * lax.rsqrt(jnp.mean(xc * xc, axis=-1, keepdims=True) + NORM_EPS)
    return (y * g.astype(jnp.float32) + b.astype(jnp.float32)).astype(x.dtype)


def alibi_slopes():
    return jnp.exp2(-8.0 * jnp.arange(1, N_HEADS + 1, dtype=jnp.float32) / N_HEADS)


def pool_mix(u, n_hist, w, scale):
    B, L, D = u.shape
    T = L - n_hist
    uf = u.astype(jnp.float32)
    cs = jnp.concatenate([jnp.zeros((B, 1, D), jnp.float32), jnp.cumsum(uf, axis=1)], axis=1)
    hi = np.arange(n_hist + 1, L + 1)
    diffs = []
    for g, win in enumerate(POOL_WINDOWS):
        lo = np.maximum(hi - win, 0)
        cnt = (hi - lo).astype(np.float32)
        sl = slice(g * POOL_GROUP, (g + 1) * POOL_GROUP)
        mean = (cs[:, hi, sl] - cs[:, lo, sl]) / cnt[None, :, None]
        diffs.append(mean - uf[:, n_hist:, sl])
    d = jnp.stack(diffs, axis=2)
    y = jnp.einsum('btgc,gcd->btgd', d, w.astype(jnp.float32)).reshape(B, T, D)
    return (y * scale.astype(jnp.float32)).astype(u.dtype)


def conv_module(h, hist, w_in, b_in, dw, dw_b, ln_g, ln_b, w_out, b_out):
    a = h @ w_in + b_in
    glu = a[..., :D_MODEL] * jax.nn.sigmoid(a[..., D_MODEL:])
    full = jnp.concatenate([hist, glu], axis=1)
    c = lax.conv_general_dilated(full, dw[:, None, :], window_strides=(1,), padding='VALID',
                                 dimension_numbers=('NWC', 'WIO', 'NWC'),
                                 feature_group_count=D_MODEL) + dw_b
    c = jax.nn.silu(layernorm(c, ln_g, ln_b))
    y = c @ w_out + b_out
    return y, full[:, -CONV_HIST:]


def swiglu(h, w_gate, w_up, w_down):
    return (jax.nn.silu(h @ w_gate) * (h @ w_up)) @ w_down


def diff_project(h, w_qkv, q_norm, k_norm):
    B, T, _ = h.shape
    q, k, v = jnp.split(h @ w_qkv, 3, axis=-1)
    q = rmsnorm(q.reshape(B, T, N_HEADS, 2, QK_DIM), q_norm)
    k = rmsnorm(k.reshape(B, T, N_HEADS, 2, QK_DIM), k_norm)
    v = v.reshape(B, T, N_HEADS, HEAD_DIM)
    return q, k, v


def diff_core(q, k, v, q_pos, k_pos, lam):
    s = jnp.einsum('bqhcd,bkhcd->bhcqk', q, k).astype(jnp.float32) * (QK_DIM ** -0.5)
    dist = (q_pos[:, None] - k_pos[None, :]).astype(jnp.float32)
    s = s - alibi_slopes()[None, :, None, None, None] * dist
    s = jnp.where(dist >= 0, s, MASK_VALUE)
    p = jax.nn.softmax(s, axis=-1)
    a = p[:, :, 0] - lam * p[:, :, 1]
    return jnp.einsum('bhqk,bkhd->bqhd', a.astype(v.dtype), v)


def diff_output(o, subln, w_o, lambda_init):
    B, T = o.shape[:2]
    o = rmsnorm(o, subln) * (1.0 - lambda_init)
    return o.reshape(B, T, D_MODEL) @ w_o


def setup_inputs(seed: int = 0) -> dict:
    key = jax.random.key(seed)
    ks = iter(jax.random.split(key, 40))
    f32 = jnp.float32

    def nrm(shape, scale):
        return jax.random.normal(next(ks), shape, f32) * scale

    n_pages = PAST_LEN // PAGE_SIZE
    n_used = DEC_BATCH * n_pages
    n_phys = n_used + n_used // 4
    perm = jax.random.permutation(next(ks), n_phys)
    page_table = perm[:n_used].reshape(DEC_BATCH, n_pages).astype(jnp.int32)
    return {
        "x_prompt": nrm((BATCH, SEQ, D_MODEL), 1.0),
        "x_sample": nrm((DEC_BATCH, DEC_SEQ, D_MODEL), 1.0),
        "state_pool": nrm((N_POOL_LAYERS, DEC_BATCH, POOL_HIST, D_MODEL), 1.0),
        "state_conv": nrm((N_CONV_LAYERS, DEC_BATCH, CONV_HIST, D_MODEL), 0.5),
        "cache_k": nrm((N_ATTN_LAYERS, n_phys, PAGE_SIZE, N_HEADS, 2, QK_DIM), 1.0),
        "cache_v": nrm((N_ATTN_LAYERS, n_phys, PAGE_SIZE, N_HEADS, HEAD_DIM), 1.0),
        "page_table": page_table,
        "norm_mix": 1.0 + nrm((DEPTH, D_MODEL), 0.05),
        "norm_ffn": 1.0 + nrm((DEPTH, D_MODEL), 0.05),
        "pool_w": nrm((N_POOL_LAYERS, N_POOL_GROUPS, POOL_GROUP, POOL_GROUP), POOL_GROUP ** -0.5),
        "pool_scale": 1.0 + nrm((N_POOL_LAYERS, D_MODEL), 0.1),
        "conv_w_in": nrm((N_CONV_LAYERS, D_MODEL, 2 * D_MODEL), D_MODEL ** -0.5),
        "conv_b_in": nrm((N_CONV_LAYERS, 2 * D_MODEL), 0.01),
        "conv_dw": nrm((N_CONV_LAYERS, CONV_WIDTH, D_MODEL), CONV_WIDTH ** -0.5),
        "conv_dw_b": nrm((N_CONV_LAYERS, D_MODEL), 0.01),
        "conv_ln_g": 1.0 + nrm((N_CONV_LAYERS, D_MODEL), 0.05),
        "conv_ln_b": nrm((N_CONV_LAYERS, D_MODEL), 0.01),
        "conv_w_out": nrm((N_CONV_LAYERS, D_MODEL, D_MODEL), D_MODEL ** -0.5),
        "conv_b_out": nrm((N_CONV_LAYERS, D_MODEL), 0.01),
        "attn_w_qkv": nrm((N_ATTN_LAYERS, D_MODEL, 3 * D_MODEL), D_MODEL ** -0.5),
        "attn_q_norm": 1.0 + nrm((N_ATTN_LAYERS, QK_DIM), 0.05),
        "attn_k_norm": 1.0 + nrm((N_ATTN_LAYERS, QK_DIM), 0.05),
        "attn_lq1": nrm((N_ATTN_LAYERS, QK_DIM), 0.1),
        "attn_lk1": nrm((N_ATTN_LAYERS, QK_DIM), 0.1),
        "attn_lq2": nrm((N_ATTN_LAYERS, QK_DIM), 0.1),
        "attn_lk2": nrm((N_ATTN_LAYERS, QK_DIM), 0.1),
        "attn_subln": 1.0 + nrm((N_ATTN_LAYERS, HEAD_DIM), 0.05),
        "attn_w_o": nrm((N_ATTN_LAYERS, D_MODEL, D_MODEL), D_MODEL ** -0.5),
        "ffn_w_gate": nrm((DEPTH, D_MODEL, D_FF), D_MODEL ** -0.5),
        "ffn_w_up": nrm((DEPTH, D_MODEL, D_FF), D_MODEL ** -0.5),
        "ffn_w_down": nrm((DEPTH, D_FF, D_MODEL), D_FF ** -0.5),
    }


def reference(x_prompt, x_sample, state_pool, state_conv, cache_k, cache_v, page_table,
              norm_mix, norm_ffn, pool_w, pool_scale,
              conv_w_in, conv_b_in, conv_dw, conv_dw_b, conv_ln_g, conv_ln_b, conv_w_out, conv_b_out,
              attn_w_qkv, attn_q_norm, attn_k_norm, attn_lq1, attn_lk1, attn_lq2, attn_lk2,
              attn_subln, attn_w_o, ffn_w_gate, ffn_w_up, ffn_w_down):
    xp, xs = x_prompt, x_sample
    Bp, Tp, _ = xp.shape
    Bs, Ts, _ = xs.shape
    past = page_table.shape[1] * PAGE_SIZE
    pool_p, pool_s, conv_p, conv_s = [], [], [], []
    kp_rows, vp_rows, ks_rows, vs_rows = [], [], [], []
    ia = ib = ic = 0
    for i in range(DEPTH):
        kind = i % N_MIXERS
        hp = rmsnorm(xp, norm_mix[i])
        hs = rmsnorm(xs, norm_mix[i])
        if kind == 0:
            yp = pool_mix(hp, 0, pool_w[ia], pool_scale[ia])
            full_s = jnp.concatenate([state_pool[ia], hs], axis=1)
            ys = pool_mix(full_s, POOL_HIST, pool_w[ia], pool_scale[ia])
            pool_p.append(hp[:, -POOL_HIST:])
            pool_s.append(full_s[:, -POOL_HIST:])
            ia += 1
        elif kind == 1:
            cargs = (conv_w_in[ib], conv_b_in[ib], conv_dw[ib], conv_dw_b[ib],
                     conv_ln_g[ib], conv_ln_b[ib], conv_w_out[ib], conv_b_out[ib])
            yp, hist_p = conv_module(hp, jnp.zeros((Bp, CONV_HIST, D_MODEL), hp.dtype), *cargs)
            ys, hist_s = conv_module(hs, state_conv[ib], *cargs)
            conv_p.append(hist_p)
            conv_s.append(hist_s)
            ib += 1
        else:
            lambda_init = 0.8 - 0.6 * math.exp(-0.3 * i)
            lam = (jnp.exp(jnp.sum(attn_lq1[ic].astype(jnp.float32) * attn_lk1[ic].astype(jnp.float32)))
                   - jnp.exp(jnp.sum(attn_lq2[ic].astype(jnp.float32) * attn_lk2[ic].astype(jnp.float32)))
                   + lambda_init)
            q, k, v = diff_project(hp, attn_w_qkv[ic], attn_q_norm[ic], attn_k_norm[ic])
            nb = Tp // QBLOCK
            qb = q.reshape(Bp, nb, QBLOCK, N_HEADS, 2, QK_DIM).transpose(1, 0, 2, 3, 4, 5)
            posb = jnp.arange(Tp).reshape(nb, QBLOCK)
            kpos = jnp.arange(Tp)
            ob = lax.map(lambda a: diff_core(a[0], k, v, a[1], kpos, lam), (qb, posb))
            o = ob.transpose(1, 0, 2, 3, 4).reshape(Bp, Tp, N_HEADS, HEAD_DIM)
            yp = diff_output(o, attn_subln[ic], attn_w_o[ic], lambda_init)
            kp_rows.append(k)
            vp_rows.append(v)
            qn, kn, vn = diff_project(hs, attn_w_qkv[ic], attn_q_norm[ic], attn_k_norm[ic])
            k_past = cache_k[ic][page_table].reshape(Bs, past, N_HEADS, 2, QK_DIM)
            v_past = cache_v[ic][page_table].reshape(Bs, past, N_HEADS, HEAD_DIM)
            k_all = jnp.concatenate([k_past, kn.astype(k_past.dtype)], axis=1)
            v_all = jnp.concatenate([v_past, vn.astype(v_past.dtype)], axis=1)
            os_ = diff_core(qn, k_all, v_all, past + jnp.arange(Ts), jnp.arange(past + Ts), lam)
            ys = diff_output(os_, attn_subln[ic], attn_w_o[ic], lambda_init)
            ks_rows.append(kn)
            vs_rows.append(vn)
            ic += 1
        xp = xp + yp
        xs = xs + ys
        xp = xp + swiglu(rmsnorm(xp, norm_ffn[i]), ffn_w_gate[i], ffn_w_up[i], ffn_w_down[i])
        xs = xs + swiglu(rmsnorm(xs, norm_ffn[i]), ffn_w_gate[i], ffn_w_up[i], ffn_w_down[i])
    return (xp, xs, jnp.stack(pool_p), jnp.stack(pool_s), jnp.stack(conv_p), jnp.stack(conv_s),
            jnp.stack(kp_rows), jnp.stack(vp_rows), jnp.stack(ks_rows), jnp.stack(vs_rows))
```

```python
import functools
import math

import jax
import jax.numpy as jnp
from jax import lax
from jax.experimental import pallas as pl
from jax.experimental.pallas import tpu as pltpu

F32 = jnp.float32
BF16 = jnp.bfloat16

D_MODEL = 1024
N_MIXERS = 3
POOL_WINDOWS = (2, 4, 8, 16)
POOL_GROUP = D_MODEL // len(POOL_WINDOWS)
POOL_HIST = max(POOL_WINDOWS) - 1
CONV_WIDTH = 31
CONV_HIST = CONV_WIDTH - 1
N_HEADS = 8
HEAD_DIM = D_MODEL // N_HEADS
QK_DIM = HEAD_DIM // 2
NORM_EPS = 1e-6
MASK_VALUE = -1e30
PAGE_SIZE = 128
LOG2E = math.log2(math.e)

SUBLANES = 8
LANES = 128
MXU_DIM = 256
VMEM_LIMIT = 56 * 1024 * 1024

POOL_HALO = 16
CONV_HALO = 32
FF_CHUNK = MXU_DIM


def _rms(x, g):
    return x * lax.rsqrt(jnp.mean(x * x, axis=-1, keepdims=True) + NORM_EPS) * g


def _dot(a, b):
    return jnp.dot(a, b, preferred_element_type=F32)


def _params(n_grid):
    return pltpu.CompilerParams(dimension_semantics=("arbitrary",) * n_grid,
                                vmem_limit_bytes=VMEM_LIMIT)


def _full(shape):
    nd = len(shape)
    return pl.BlockSpec(shape, lambda *_: (0,) * nd, pipeline_mode=pl.Buffered(1))


def _ffn_body(*refs, has_pre):
    if has_pre:
        x_ref, u_ref, wpre_ref, bpre_ref, g_ref, wg_ref, wu_ref, wd_ref, o_ref, h_ref, z_ref = refs
        x = x_ref[...] + _dot(u_ref[...], wpre_ref[...]) + bpre_ref[...]
    else:
        x_ref, g_ref, wg_ref, wu_ref, wd_ref, o_ref, h_ref, z_ref = refs
        x = x_ref[...]
    h_ref[...] = _rms(x, g_ref[...]).astype(BF16)
    d_ff = wg_ref.shape[1]
    for c in range(d_ff // FF_CHUNK):
        cs = slice(c * FF_CHUNK, (c + 1) * FF_CHUNK)
        h = h_ref[...]
        a = _dot(h, wg_ref[:, cs])
        b = _dot(h, wu_ref[:, cs])
        z_ref[:, cs] = (a * jax.nn.sigmoid(a) * b).astype(BF16)
    o_ref[...] = x + _dot(z_ref[...], wd_ref[...])


def _ffn(x, g, wg, wu, wd, pre=None, *, tm):
    m = x.shape[0]
    d_ff = wg.shape[1]
    row = pl.BlockSpec((tm, D_MODEL), lambda i: (i, 0))
    args, specs = [x], [row]
    if pre is not None:
        u, wpre, bpre = pre
        args += [u, wpre, bpre]
        specs += [row, _full(wpre.shape), _full(bpre.shape)]
    args += [g, wg, wu, wd]
    specs += [_full(g.shape), _full(wg.shape), _full(wu.shape), _full(wd.shape)]
    return pl.pallas_call(
        functools.partial(_ffn_body, has_pre=pre is not None),
        grid=(m // tm,),
        in_specs=specs,
        out_specs=row,
        out_shape=jax.ShapeDtypeStruct((m, D_MODEL), F32),
        scratch_shapes=[pltpu.VMEM((tm, D_MODEL), BF16), pltpu.VMEM((tm, d_ff), BF16)],
        compiler_params=_params(1),
        name="ffn_pre" if pre is not None else "ffn",
    )(*args)


def _pool_group_out(x, h, s, cnt, g, w_ref, sc_ref, o_ref):
    cs = slice(g * POOL_GROUP, (g + 1) * POOL_GROUP)
    d = s / cnt - h[:, cs]
    y = _dot(d.astype(BF16), w_ref[g])
    o_ref[:, cs] = x[:, cs] + y * sc_ref[:, cs]


def _pool_prompt_body(x_ref, g_ref, w_ref, sc_ref, o_ref, st_ref, buf_ref, *, tm, tiles_per_seq):
    tib = pl.program_id(0) % tiles_per_seq

    @pl.when(tib == 0)
    def _():
        buf_ref[0:POOL_HALO, :] = jnp.zeros((POOL_HALO, D_MODEL), F32)

    @pl.when(tib != 0)
    def _():
        buf_ref[0:POOL_HALO, :] = buf_ref[tm:tm + POOL_HALO, :]

    x = x_ref[...]
    h = _rms(x, g_ref[...])
    buf_ref[POOL_HALO:POOL_HALO + tm, :] = h
    st_ref[0] = h[tm - POOL_HALO:, :]
    t = tib * tm + lax.broadcasted_iota(jnp.int32, (tm, POOL_GROUP), 0)
    for g, win in enumerate(POOL_WINDOWS):
        cs = slice(g * POOL_GROUP, (g + 1) * POOL_GROUP)
        s = h[:, cs]
        for j in range(1, win):
            s = s + buf_ref[POOL_HALO - j:POOL_HALO - j + tm, cs]
        cnt = jnp.minimum(t + 1, win).astype(F32)
        _pool_group_out(x, h, s, cnt, g, w_ref, sc_ref, o_ref)


def _pool_prompt(x, g, w, sc, *, n_seq, tm):
    m = x.shape[0]
    tiles_per_seq = m // n_seq // tm
    row = pl.BlockSpec((tm, D_MODEL), lambda i: (i, 0))
    return pl.pallas_call(
        functools.partial(_pool_prompt_body, tm=tm, tiles_per_seq=tiles_per_seq),
        grid=(m // tm,),
        in_specs=[row, _full(g.shape), _full(w.shape), _full(sc.shape)],
        out_specs=[row, pl.BlockSpec((1, POOL_HALO, D_MODEL), lambda i: (i // tiles_per_seq, 0, 0))],
        out_shape=[jax.ShapeDtypeStruct((m, D_MODEL), F32),
                   jax.ShapeDtypeStruct((n_seq, POOL_HALO, D_MODEL), F32)],
        scratch_shapes=[pltpu.VMEM((POOL_HALO + tm, D_MODEL), F32)],
        compiler_params=_params(1),
        name="pool_prompt",
    )(x, g, w, sc)


def _pool_sample_body(x_ref, st_ref, g_ref, w_ref, sc_ref, o_ref, h_ref):
    x = x_ref[...]
    h = _rms(x, g_ref[...])
    h_ref[...] = h
    for g, win in enumerate(POOL_WINDOWS):
        cs = slice(g * POOL_GROUP, (g + 1) * POOL_GROUP)
        s = h[:, cs]
        for j in range(1, win):
            s = s + st_ref[POOL_HIST - j][:, cs]
        _pool_group_out(x, h, s, float(win), g, w_ref, sc_ref, o_ref)


def _pool_sample(x, st, g, w, sc):
    m = x.shape[0]
    return pl.pallas_call(
        _pool_sample_body,
        out_shape=[jax.ShapeDtypeStruct((m, D_MODEL), F32), jax.ShapeDtypeStruct((m, D_MODEL), F32)],
        compiler_params=pltpu.CompilerParams(vmem_limit_bytes=VMEM_LIMIT),
        name="pool_sample",
    )(x, st, g, w, sc)


def _glu(x, g_ref, win_ref, bin_ref):
    h = _rms(x, g_ref[...]).astype(BF16)
    a = _dot(h, win_ref[...]) + bin_ref[...]
    return a[:, :D_MODEL] * jax.nn.sigmoid(a[:, D_MODEL:])


def _ln_silu(c, lng_ref, lnb_ref):
    mu = jnp.mean(c, axis=-1, keepdims=True)
    xc = c - mu
    y = xc * lax.rsqrt(jnp.mean(xc * xc, axis=-1, keepdims=True) + NORM_EPS)
    y = y * lng_ref[...] + lnb_ref[...]
    return y * jax.nn.sigmoid(y)


def _conv_prompt_body(x_ref, g_ref, win_ref, bin_ref, dw_ref, dwb_ref, lng_ref, lnb_ref,
                      u_ref, st_ref, buf_ref, *, tm, tiles_per_seq, rows):
    tib = pl.program_id(0) % tiles_per_seq

    @pl.when(tib == 0)
    def _():
        buf_ref[0:CONV_HALO, :] = jnp.zeros((CONV_HALO, D_MODEL), F32)

    @pl.when(tib != 0)
    def _():
        buf_ref[0:CONV_HALO, :] = buf_ref[tm:tm + CONV_HALO, :]

    glu = _glu(x_ref[...], g_ref, win_ref, bin_ref)
    buf_ref[CONV_HALO:CONV_HALO + tm, :] = glu
    st_ref[0] = glu[tm - CONV_HALO:, :]
    first = CONV_HALO - CONV_HIST
    for r in range(tm // rows):
        acc = jnp.broadcast_to(dwb_ref[...], (rows, D_MODEL))
        for j in range(CONV_WIDTH):
            lo = r * rows + first + j
            acc = acc + dw_ref[j:j + 1, :] * buf_ref[lo:lo + rows, :]
        u_ref[r * rows:(r + 1) * rows, :] = _ln_silu(acc, lng_ref, lnb_ref).astype(BF16)


def _conv_prompt(x, g, w_in, b_in, dw, dwb, lng, lnb, *, n_seq, tm):
    m = x.shape[0]
    tiles_per_seq = m // n_seq // tm
    row = pl.BlockSpec((tm, D_MODEL), lambda i: (i, 0))
    consts = [g, w_in, b_in, dw, dwb, lng, lnb]
    return pl.pallas_call(
        functools.partial(_conv_prompt_body, tm=tm, tiles_per_seq=tiles_per_seq, rows=32),
        grid=(m // tm,),
        in_specs=[row] + [_full(c.shape) for c in consts],
        out_specs=[row, pl.BlockSpec((1, CONV_HALO, D_MODEL), lambda i: (i // tiles_per_seq, 0, 0))],
        out_shape=[jax.ShapeDtypeStruct((m, D_MODEL), BF16),
                   jax.ShapeDtypeStruct((n_seq, CONV_HALO, D_MODEL), F32)],
        scratch_shapes=[pltpu.VMEM((CONV_HALO + tm, D_MODEL), F32)],
        compiler_params=_params(1),
        name="conv_prompt",
    )(x, *consts)


def _conv_sample_body(x_ref, st_ref, g_ref, win_ref, bin_ref, dw_ref, dwb_ref, lng_ref, lnb_ref,
                      u_ref, glu_ref):
    glu = _glu(x_ref[...], g_ref, win_ref, bin_ref)
    glu_ref[...] = glu
    acc = dwb_ref[...] + dw_ref[CONV_HIST:CONV_WIDTH, :] * glu
    for j in range(CONV_HIST):
        acc = acc + dw_ref[j:j + 1, :] * st_ref[j]
    u_ref[...] = _ln_silu(acc, lng_ref, lnb_ref).astype(BF16)


def _conv_sample(x, st, g, w_in, b_in, dw, dwb, lng, lnb):
    m = x.shape[0]
    return pl.pallas_call(
        _conv_sample_body,
        out_shape=[jax.ShapeDtypeStruct((m, D_MODEL), BF16), jax.ShapeDtypeStruct((m, D_MODEL), F32)],
        compiler_params=pltpu.CompilerParams(vmem_limit_bytes=VMEM_LIMIT),
        name="conv_sample",
    )(x, st, g, w_in, b_in, dw, dwb, lng, lnb)


def _qkv_body(x_ref, g_ref, w_ref, qn_ref, kn_ref, e_ref, q_ref, k_ref, v_ref, kb_ref, vb_ref):
    h = _rms(x_ref[...], g_ref[...]).astype(BF16)

    def chunk_norm(t, nw_ref):
        t2 = (t * t).astype(BF16)
        ss = jnp.concatenate(
            [_dot(t2[:, c * MXU_DIM:(c + 1) * MXU_DIM], e_ref[...]) for c in range(D_MODEL // MXU_DIM)],
            axis=1)
        return t * lax.rsqrt(ss * (1.0 / QK_DIM) + NORM_EPS) * nw_ref[...]

    q = chunk_norm(_dot(h, w_ref[:, :D_MODEL]), qn_ref)
    q_ref[...] = (q * (QK_DIM ** -0.5 * LOG2E)).astype(BF16)
    k = chunk_norm(_dot(h, w_ref[:, D_MODEL:2 * D_MODEL]), kn_ref)
    k_ref[...] = k
    kb_ref[...] = k.astype(BF16)
    v = _dot(h, w_ref[:, 2 * D_MODEL:])
    v_ref[...] = v
    vb_ref[...] = v.astype(BF16)


def _qkv(x, g, w, qn, kn, ones_bd, *, tm):
    m = x.shape[0]
    row = pl.BlockSpec((tm, D_MODEL), lambda i: (i, 0))
    consts = [g, w, qn, kn, ones_bd]
    sd = jax.ShapeDtypeStruct
    return pl.pallas_call(
        _qkv_body,
        grid=(m // tm,),
        in_specs=[row] + [_full(c.shape) for c in consts],
        out_specs=[row] * 5,
        out_shape=[sd((m, D_MODEL), BF16), sd((m, D_MODEL), F32), sd((m, D_MODEL), F32),
                   sd((m, D_MODEL), BF16), sd((m, D_MODEL), BF16)],
        compiler_params=_params(1),
        name="qkv",
    )(x, *consts)


def _lambda(lq1_ref, lk1_ref, lq2_ref, lk2_ref, lambda_init):
    s1 = jnp.sum(lq1_ref[...] * lk1_ref[...], axis=-1, keepdims=True)
    s2 = jnp.sum(lq2_ref[...] * lk2_ref[...], axis=-1, keepdims=True)
    return jnp.exp(s1) - jnp.exp(s2) + lambda_init


def _attn_prompt_body(slopes_ref, q_ref, k_ref, v_ref, lq1_ref, lk1_ref, lq2_ref, lk2_ref, sub_ref,
                      o_ref, bias_ref, m_ref, l_ref, acc_ref, *, tq, lambda_init):
    qi = pl.program_id(2)
    slope = slopes_ref[pl.program_id(1)] * LOG2E
    r = lax.broadcasted_iota(jnp.int32, (tq, tq), 0)
    c = lax.broadcasted_iota(jnp.int32, (tq, tq), 1)
    bias = slope * (c - r).astype(F32)
    bias_ref[0] = bias
    bias_ref[1] = jnp.where(c > r, MASK_VALUE, bias)

    q = q_ref[...]
    lane = lax.broadcasted_iota(jnp.int32, (tq, HEAD_DIM), 1)
    zero = jnp.zeros_like(q)
    qs = jnp.concatenate([jnp.where(lane < QK_DIM, q, zero), jnp.where(lane >= QK_DIM, q, zero)], axis=0)

    m_ref[...] = jnp.full(m_ref.shape, MASK_VALUE, F32)
    l_ref[...] = jnp.zeros(l_ref.shape, F32)
    acc_ref[...] = jnp.zeros(acc_ref.shape, F32)

    def block(ki, bias_idx):
        k0 = pl.multiple_of(ki * tq, tq)
        kb = k_ref[pl.ds(k0, tq), :]
        vb = v_ref[pl.ds(k0, tq), :]
        s = lax.dot_general(qs, kb, (((1,), (1,)), ((), ())), preferred_element_type=F32)
        s = s.reshape(2, tq, tq) + bias_ref[bias_idx]
        off = slope * ((ki - qi) * tq).astype(F32)
        m_prev = m_ref[...]
        m_new = jnp.maximum(m_prev, jnp.max(s, axis=-1, keepdims=True) + off)
        alpha = jnp.exp2(m_prev - m_new)
        p = jnp.exp2(s - (m_new - off))
        l_ref[...] = alpha * l_ref[...] + jnp.sum(p, axis=-1, keepdims=True)
        pv = _dot(p.reshape(2 * tq, tq).astype(BF16), vb)
        acc_ref[...] = alpha * acc_ref[...] + pv.reshape(2, tq, HEAD_DIM)
        m_ref[...] = m_new

    def off_diag(ki, carry):
        block(ki, 0)
        return carry

    lax.fori_loop(0, qi, off_diag, 0)
    block(qi, 1)

    lam = _lambda(lq1_ref, lk1_ref, lq2_ref, lk2_ref, lambda_init)
    o = acc_ref[0] / l_ref[0] - lam * (acc_ref[1] / l_ref[1])
    o = _rms(o, sub_ref[...]) * (1.0 - lambda_init)
    o_ref[...] = o.astype(BF16)


def _attn_prompt(q, k, v, slopes, lq1, lk1, lq2, lk2, sub, *, n_seq, tq, lambda_init):
    m = q.shape[0]
    t = m // n_seq
    nq = t // tq
    qspec = pl.BlockSpec((tq, HEAD_DIM), lambda b, h, i: (b * nq + i, h))
    kvspec = pl.BlockSpec((t, HEAD_DIM), lambda b, h, i: (b, h))
    small = [lq1, lk1, lq2, lk2, sub]
    return pl.pallas_call(
        functools.partial(_attn_prompt_body, tq=tq, lambda_init=lambda_init),
        grid=(n_seq, N_HEADS, nq),
        in_specs=[pl.BlockSpec(memory_space=pltpu.SMEM), qspec, kvspec, kvspec]
                 + [pl.BlockSpec(a.shape, lambda b, h, i: (0, 0)) for a in small],
        out_specs=qspec,
        out_shape=jax.ShapeDtypeStruct((m, D_MODEL), BF16),
        scratch_shapes=[pltpu.VMEM((2, tq, tq), F32), pltpu.VMEM((2, tq, 1), F32),
                        pltpu.VMEM((2, tq, 1), F32), pltpu.VMEM((2, tq, HEAD_DIM), F32)],
        compiler_params=_params(3),
        name="attn_prompt",
    )(slopes, q, k, v, *small)


def _attn_decode_body(pt_ref, q_ref, kc_ref, vc_ref, kn_ref, vn_ref, seg_ref, head_ref, slope_ref,
                      lq1_ref, lk1_ref, lq2_ref, lk2_ref, sub_ref, o_ref,
                      qr_ref, m_ref, l_ref, acc_ref, *, n_pages, lambda_init):
    del pt_ref
    p_idx = pl.program_id(1)

    @pl.when(p_idx == 0)
    def _():
        qr_ref[...] = seg_ref[...] * q_ref[0].astype(F32)
        m_ref[...] = jnp.full(m_ref.shape, MASK_VALUE, F32)
        l_ref[...] = jnp.zeros(l_ref.shape, F32)
        acc_ref[...] = jnp.zeros(acc_ref.shape, F32)

    def update(s, value_fn):
        m_prev = m_ref[...]
        m_new = jnp.maximum(m_prev, jnp.max(s, axis=-1, keepdims=True))
        alpha = jnp.exp2(m_prev - m_new)
        p = jnp.exp2(s - m_new)
        l_ref[...] = alpha * l_ref[...] + jnp.sum(p, axis=-1, keepdims=True)
        acc_ref[...] = alpha * acc_ref[...] + value_fn(p)
        m_ref[...] = m_new

    past = n_pages * PAGE_SIZE
    kb = kc_ref[0].astype(BF16)
    vb = vc_ref[0].astype(BF16)
    s = lax.dot_general(qr_ref[...].astype(BF16), kb, (((1,), (1,)), ((), ())),
                        preferred_element_type=F32)
    kpos = p_idx * PAGE_SIZE + lax.broadcasted_iota(jnp.int32, s.shape, 1)
    s = s - (slope_ref[...] * LOG2E) * (past - kpos).astype(F32)
    update(s, lambda p: _dot(p.astype(BF16), vb))

    @pl.when(p_idx == n_pages - 1)
    def _():
        s_new = jnp.sum(qr_ref[...] * kn_ref[0], axis=-1, keepdims=True)
        update(s_new, lambda p: p * vn_ref[0])
        lam = _lambda(lq1_ref, lk1_ref, lq2_ref, lk2_ref, lambda_init)
        row = lax.broadcasted_iota(jnp.int32, l_ref.shape, 0)
        coef = jnp.where(row < N_HEADS, 1.0, -lam)
        w = acc_ref[...] * (coef / l_ref[...]) * head_ref[...]
        t = w[:N_HEADS] + w[N_HEADS:]
        t = t * lax.rsqrt(jnp.sum(t * t, axis=-1, keepdims=True) * (1.0 / HEAD_DIM) + NORM_EPS)
        o = jnp.sum(t, axis=0, keepdims=True) * sub_ref[...] * (1.0 - lambda_init)
        o_ref[0] = o.astype(BF16)


def _attn_decode(page_table, q, kc, vc, kn, vn, seg, head, slope16, lq1, lk1, lq2, lk2, sub_t,
                 *, lambda_init):
    n_seq, n_pages = page_table.shape
    row3 = pl.BlockSpec((1, 1, D_MODEL), lambda s, p, pt: (s, 0, 0))
    page = pl.BlockSpec((1, PAGE_SIZE, D_MODEL), lambda s, p, pt: (pt[s * n_pages + p], 0, 0))
    consts = [seg, head, slope16, lq1, lk1, lq2, lk2, sub_t]
    nr = 2 * N_HEADS
    grid_spec = pltpu.PrefetchScalarGridSpec(
        num_scalar_prefetch=1,
        grid=(n_seq, n_pages),
        in_specs=[row3, page, page, row3, row3]
                 + [pl.BlockSpec(a.shape, lambda s, p, pt: (0, 0)) for a in consts],
        out_specs=row3,
        scratch_shapes=[pltpu.VMEM((nr, D_MODEL), F32), pltpu.VMEM((nr, 1), F32),
                        pltpu.VMEM((nr, 1), F32), pltpu.VMEM((nr, D_MODEL), F32)],
    )
    return pl.pallas_call(
        functools.partial(_attn_decode_body, n_pages=n_pages, lambda_init=lambda_init),
        grid_spec=grid_spec,
        out_shape=jax.ShapeDtypeStruct((n_seq, 1, D_MODEL), BF16),
        compiler_params=_params(2),
        name="attn_decode",
    )(page_table.reshape(-1), q, kc, vc, kn, vn, *consts)


def _row(v):
    return v.reshape(1, -1).astype(F32)


def kernel(x_prompt, x_sample, state_pool, state_conv, cache_k, cache_v, page_table, norm_mix, norm_ffn, pool_w, pool_scale, conv_w_in, conv_b_in, conv_dw, conv_dw_b, conv_ln_g, conv_ln_b, conv_w_out, conv_b_out, attn_w_qkv, attn_q_norm, attn_k_norm, attn_lq1, attn_lk1, attn_lq2, attn_lk2, attn_subln, attn_w_o, ffn_w_gate, ffn_w_up, ffn_w_down):
    bp, tp, _ = x_prompt.shape
    bs, ts, _ = x_sample.shape
    assert ts == 1 and tp % 512 == 0
    depth = norm_mix.shape[0]
    xp = x_prompt.reshape(bp * tp, D_MODEL)
    xs = x_sample.reshape(bs, D_MODEL)
    tm_p, tm_s = 512, bs

    lane = jnp.arange(D_MODEL)
    ones_bd = (jnp.arange(MXU_DIM)[:, None] // QK_DIM == jnp.arange(MXU_DIM)[None, :] // QK_DIM).astype(BF16)
    rows = jnp.arange(2 * N_HEADS)
    seg = (lane[None, :] // QK_DIM == (2 * (rows % N_HEADS) + rows // N_HEADS)[:, None]).astype(F32)
    head = (lane[None, :] // HEAD_DIM == (rows % N_HEADS)[:, None]).astype(F32)
    slopes = jnp.exp2(-8.0 * jnp.arange(1, N_HEADS + 1, dtype=F32) / N_HEADS)
    slope16 = jnp.tile(slopes, 2).reshape(2 * N_HEADS, 1)
    zero_bias = jnp.zeros((1, D_MODEL), F32)

    pool_p, pool_s, conv_p, conv_s = [], [], [], []
    kp_rows, vp_rows, ks_rows, vs_rows = [], [], [], []
    ia = ib = ic = 0
    for i in range(depth):
        kind = i % N_MIXERS
        g_mix, g_ffn = _row(norm_mix[i]), _row(norm_ffn[i])
        wg, wu, wd = (w[i].astype(BF16) for w in (ffn_w_gate, ffn_w_up, ffn_w_down))
        pre_p = pre_s = None
        if kind == 0:
            w, sc = pool_w[ia].astype(BF16), _row(pool_scale[ia])
            xp, st_p = _pool_prompt(xp, g_mix, w, sc, n_seq=bp, tm=tm_p)
            pool_p.append(st_p[:, POOL_HALO - POOL_HIST:])
            st = jnp.transpose(state_pool[ia], (1, 0, 2))
            xs, hs = _pool_sample(xs, st, g_mix, w, sc)
            pool_s.append(jnp.concatenate([state_pool[ia][:, 1:], hs[:, None]], axis=1))
            ia += 1
        elif kind == 1:
            cargs = (conv_w_in[ib].astype(BF16), _row(conv_b_in[ib]), conv_dw[ib], _row(conv_dw_b[ib]),
                     _row(conv_ln_g[ib]), _row(conv_ln_b[ib]))
            w_out, b_out = conv_w_out[ib].astype(BF16), _row(conv_b_out[ib])
            up, st_p = _conv_prompt(xp, g_mix, *cargs, n_seq=bp, tm=256)
            conv_p.append(st_p[:, CONV_HALO - CONV_HIST:])
            st = jnp.transpose(state_conv[ib], (1, 0, 2))
            us, glu_s = _conv_sample(xs, st, g_mix, *cargs)
            conv_s.append(jnp.concatenate([state_conv[ib][:, 1:], glu_s[:, None]], axis=1))
            pre_p, pre_s = (up, w_out, b_out), (us, w_out, b_out)
            ib += 1
        else:
            lambda_init = 0.8 - 0.6 * math.exp(-0.3 * i)
            w_qkv, w_o = attn_w_qkv[ic].astype(BF16), attn_w_o[ic].astype(BF16)
            qn = _row(jnp.tile(attn_q_norm[ic], D_MODEL // QK_DIM))
            kn = _row(jnp.tile(attn_k_norm[ic], D_MODEL // QK_DIM))
            lams = [_row(a[ic]) for a in (attn_lq1, attn_lk1, attn_lq2, attn_lk2)]
            sub = _row(attn_subln[ic])
            q, k, v, kb, vb = _qkv(xp, g_mix, w_qkv, qn, kn, ones_bd, tm=tm_p)
            op = _attn_prompt(q, kb, vb, slopes, *lams, sub, n_seq=bp, tq=512, lambda_init=lambda_init)
            kp_rows.append(k.reshape(bp, tp, N_HEADS, 2, QK_DIM))
            vp_rows.append(v.reshape(bp, tp, N_HEADS, HEAD_DIM))
            qs_, ks_, vs_, _, _ = _qkv(xs, g_mix, w_qkv, qn, kn, ones_bd, tm=tm_s)
            n_phys = cache_k.shape[1]
            os_ = _attn_decode(page_table, qs_.reshape(bs, 1, D_MODEL),
                               cache_k[ic].reshape(n_phys, PAGE_SIZE, D_MODEL),
                               cache_v[ic].reshape(n_phys, PAGE_SIZE, D_MODEL),
                               ks_.reshape(bs, 1, D_MODEL), vs_.reshape(bs, 1, D_MODEL),
                               seg, head, slope16, *lams, _row(jnp.tile(attn_subln[ic], N_HEADS)),
                               lambda_init=lambda_init)
            ks_rows.append(ks_.reshape(bs, ts, N_HEADS, 2, QK_DIM))
            vs_rows.append(vs_.reshape(bs, ts, N_HEADS, HEAD_DIM))
            pre_p, pre_s = (op, w_o, zero_bias), (os_.reshape(bs, D_MODEL), w_o, zero_bias)
            ic += 1
        xp = _ffn(xp, g_ffn, wg, wu, wd, pre_p, tm=tm_p)
        xs = _ffn(xs, g_ffn, wg, wu, wd, pre_s, tm=tm_s)
    return (xp.reshape(bp, tp, D_MODEL), xs.reshape(bs, ts, D_MODEL),
            jnp.stack(pool_p), jnp.stack(pool_s), jnp.stack(conv_p), jnp.stack(conv_s),
            jnp.stack(kp_rows), jnp.stack(vp_rows), jnp.stack(ks_rows), jnp.stack(vs_rows))
```
